```python
import jax
import jax.numpy as jnp
from jax import lax
import numpy as np

D_MODEL = 1024
BATCH = 8
SEQ = 2048
DEPTH = 2

GRID_W = 64
CTX_LEN = 256
HG_DK = 128
HG_WIDTH = D_MODEL // 2
HG_HEADS = HG_WIDTH // HG_DK
HG_CHUNK = 64
SC_WIDTH = D_MODEL // 4
SC_TAPS = 3
SG_WIDTH = D_MODEL // 4
SG_HEADS = 4
SG_HEAD_DIM = SG_WIDTH // SG_HEADS
SG_CHUNK = 128
MIX_WIDTH = HG_WIDTH + SC_WIDTH + SG_WIDTH
HG_STATE_COLS = 3 * HG_WIDTH
IN_COLS = 5 * HG_WIDTH + 3 * SC_WIDTH + 2 * SG_WIDTH
D_FF = 2816
N_MOD = 6
ALPHA = (2 * DEPTH) ** 0.25
BETA = (8 * DEPTH) ** -0.25
EPS = 1e-6
TINY = 1e-30

O_Q = 3 * HG_WIDTH
O_G = 4 * HG_WIDTH
O_SC = 5 * HG_WIDTH
O_SG = O_SC + 3 * SC_WIDTH

kernel_name = 'hybrid_hgrn2_shortconv_chunkmlp_dit'

F32 = jnp.float32


def layer_norm(x, g=None, b=None):
    xf = x.astype(F32)
    xc = xf - jnp.mean(xf, axis=-1, keepdims=True)
    y = xc * lax.rsqrt(jnp.mean(xc * xc, axis=-1, keepdims=True) + EPS)
    if g is not None:
        y = y * g.astype(F32) + b.astype(F32)
    return y.astype(x.dtype)


def modulate(x, shift, scale):
    return layer_norm(x) * (1.0 + scale) + shift


def hg_lower_bounds(lb_logits):
    p = jax.nn.softmax(lb_logits.astype(F32), axis=0)
    return jnp.cumsum(p, axis=0) - p[:1]


def hg_forget(z, lb):
    z = z.astype(F32)
    k = (1.0 - lb) * jax.nn.sigmoid(-z)
    logf = jnp.log(jnp.maximum(lb + (1.0 - lb) * jax.nn.sigmoid(z), TINY))
    return k, logf


def split_heads(a):
    return a.reshape(a.shape[0], a.shape[1], HG_HEADS, HG_DK)


def flip(a):
    return a[:, ::-1]


def hg_state_inputs(p_state, lb_f, lb_b):
    i, z_f, z_b = jnp.split(p_state.astype(F32), 3, axis=-1)
    k_f, lf_f = hg_forget(z_f, lb_f)
    k_b, lf_b = hg_forget(z_b, lb_b)
    return (split_heads(i), split_heads(k_f), split_heads(k_b), split_heads(lf_f), split_heads(lf_b))


def gla_chunk_scan(q, k, v, logf, s0):
    bsz, t, h, dk = q.shape
    dv = v.shape[-1]
    n = t // HG_CHUNK

    def to_chunks(a):
        return a.astype(F32).reshape(bsz, n, HG_CHUNK, h, a.shape[-1]).transpose(1, 0, 3, 2, 4)

    causal = jnp.tril(jnp.ones((HG_CHUNK, HG_CHUNK), bool))[:, :, None]

    def step(s, inp):
        qc, kc, vc, gc = inp
        b = jnp.cumsum(gc, axis=2)
        inter = jnp.einsum('bhtk,bhkv->bhtv', qc * jnp.exp(b), s)
        diff = b[:, :, :, None, :] - b[:, :, None, :, :]
        decay = jnp.where(causal, jnp.exp(jnp.minimum(diff, 0.0)), 0.0)
        scores = jnp.einsum('bhtk,bhtsk,bhsk->bhts', qc, decay, kc)
        intra = jnp.einsum('bhts,bhsv->bhtv', scores, vc)
        b_last = b[:, :, -1:, :]
        s_new = jnp.exp(b_last[:, :, 0, :, None]) * s + jnp.einsum('bhsk,bhsv->bhkv', kc * jnp.exp(jnp.minimum(b_last - b, 0.0)), vc)
        return s_new, inter + intra

    _, o = lax.scan(step, s0.astype(F32), (to_chunks(q), to_chunks(k), to_chunks(v), to_chunks(logf)))
    return o.transpose(1, 0, 3, 2, 4).reshape(bsz, t, h, dv)


def gla_final_state(k, v, logf):
    suffix = jnp.minimum(lax.cumsum(logf, axis=1, reverse=True) - logf, 0.0)
    return jnp.einsum('bthk,bthv->bhkv', k * jnp.exp(suffix), v)


def hgrn2_bidir(q, v, k_f, k_b, lf_f, lf_b, s0_f, s0_b):
    o_f = gla_chunk_scan(q, k_f, v, lf_f, s0_f)
    o_b = flip(gla_chunk_scan(flip(q), flip(k_b), flip(v), flip(lf_b), s0_b))
    return o_f + o_b


def conv1d_centred(x, w):
    xp = jnp.pad(x, ((0, 0), (1, 1), (0, 0)))
    return xp[:, :-2] * w[0] + xp[:, 1:-1] * w[1] + xp[:, 2:] * w[2]


def chunk_spatial_gate(u, v, ln_g, ln_b, w_s, b_s):
    bsz, t, _ = v.shape
    n = t // SG_CHUNK
    vh = layer_norm(v.reshape(bsz, t, SG_HEADS, SG_HEAD_DIM),
                    ln_g.reshape(SG_HEADS, SG_HEAD_DIM), ln_b.reshape(SG_HEADS, SG_HEAD_DIM))
    vh = vh.reshape(bsz, n, SG_CHUNK, SG_HEADS, SG_HEAD_DIM)
    mixed = jnp.einsum('gts,bnsgc->bntgc', w_s, vh) + b_s.T[:, :, None]
    return u * mixed.reshape(bsz, t, SG_WIDTH)


def token_mixers(p, hg_in, s0_f, s0_b, hg_norm_g, sc_w, sg_ln_g, sg_ln_b, sg_w, sg_b):
    v, k_f, k_b, lf_f, lf_b = hg_in
    q = jax.nn.silu(split_heads(p[..., O_Q:O_G].astype(F32))) * HG_DK ** -0.5
    o = hgrn2_bidir(q, v, k_f, k_b, lf_f, lf_b, s0_f, s0_b)
    o = o * lax.rsqrt(jnp.mean(o * o, axis=-1, keepdims=True) + EPS) * hg_norm_g.astype(F32)
    hg = (o.reshape(o.shape[0], o.shape[1], HG_WIDTH) * jax.nn.silu(p[..., O_G:O_SC].astype(F32))).astype(p.dtype)
    gate_b, gate_c, h_sc = jnp.split(p[..., O_SC:O_SG], 3, axis=-1)
    sc = gate_b * conv1d_centred(gate_c * h_sc, sc_w)
    u, v_sg = jnp.split(p[..., O_SG:], 2, axis=-1)
    sg = chunk_spatial_gate(u, v_sg, sg_ln_g, sg_ln_b, sg_w, sg_b)
    return jnp.concatenate([hg, sc, sg], axis=-1)


def dwconv3x3(img, w):
    rows, cols = img.shape[1], img.shape[2]
    pad = jnp.pad(img, ((0, 0), (1, 1), (1, 1), (0, 0)))
    out = pad[:, 0:rows, 0:cols] * w[0, 0]
    for di in range(3):
        for dj in range(3):
            if di or dj:
                out = out + pad[:, di:di + rows, dj:dj + cols] * w[di, dj]
    return out


def conv_ffn(h, w_up, conv_w, conv_b, w_down, grid_rows):
    a, g = jnp.split(h @ w_up, 2, axis=-1)
    bsz, t, f = a.shape
    a = dwconv3x3(a.reshape(bsz, grid_rows, t // grid_rows, f), conv_w).reshape(bsz, t, f) + conv_b
    return (jax.nn.gelu(a, approximate=False) * g) @ w_down


def _normal(k, shape, scale):
    return jax.random.normal(k, shape, F32) * scale


def setup_inputs(seed: int = 0) -> dict:
    key = jax.random.key(seed)
    ks = jax.random.split(key, 23)
    return {
        'x': _normal(ks[0], (BATCH, SEQ, D_MODEL), 1.0),
        'c': _normal(ks[1], (BATCH, D_MODEL), 1.0),
        'ctx': _normal(ks[2], (BATCH, CTX_LEN, D_MODEL), 1.0),
        'c_ctx': _normal(ks[3], (D_MODEL,), 1.0),
        'ada_w': _normal(ks[4], (DEPTH, D_MODEL, N_MOD * D_MODEL), 0.5 * D_MODEL ** -0.5),
        'ada_b': _normal(ks[5], (DEPTH, N_MOD * D_MODEL), 0.02),
        'w_in': _normal(ks[6], (DEPTH, D_MODEL, IN_COLS), D_MODEL ** -0.5),
        'hg_lb': _normal(ks[7], (DEPTH, 2, HG_WIDTH), 1.0),
        'hg_norm_g': 1.0 + _normal(ks[8], (DEPTH, HG_DK), 0.02),
        'sc_conv_w': _normal(ks[9], (DEPTH, SC_TAPS, SC_WIDTH), SC_TAPS ** -0.5),
        'sg_ln_g': 1.0 + _normal(ks[10], (DEPTH, SG_WIDTH), 0.02),
        'sg_ln_b': _normal(ks[11], (DEPTH, SG_WIDTH), 0.02),
        'sg_w': _normal(ks[12], (DEPTH, SG_HEADS, SG_CHUNK, SG_CHUNK), 0.5 * SG_CHUNK ** -0.5),
        'sg_b': 1.0 + _normal(ks[13], (DEPTH, SG_HEADS, SG_CHUNK), 0.02),
        'w_out': _normal(ks[14], (DEPTH, MIX_WIDTH, D_MODEL), BETA * MIX_WIDTH ** -0.5),
        'ln1_g': 1.0 + _normal(ks[15], (DEPTH, D_MODEL), 0.02),
        'ln1_b': _normal(ks[16], (DEPTH, D_MODEL), 0.02),
        'ffn_up': _normal(ks[17], (DEPTH, D_MODEL, 2 * D_FF), D_MODEL ** -0.5),
        'ffn_conv_w': _normal(ks[18], (DEPTH, 3, 3, D_FF), 1.0 / 3.0),
        'ffn_conv_b': _normal(ks[19], (DEPTH, D_FF), 0.02),
        'ffn_down': _normal(ks[20], (DEPTH, D_FF, D_MODEL), BETA * D_FF ** -0.5),
        'ln2_g': 1.0 + _normal(ks[21], (DEPTH, D_MODEL), 0.02),
        'ln2_b': _normal(ks[22], (DEPTH, D_MODEL), 0.02),
    }


def reference(x, c, ctx, c_ctx, ada_w, ada_b, w_in, hg_lb, hg_norm_g, sc_conv_w, sg_ln_g, sg_ln_b,
              sg_w, sg_b, w_out, ln1_g, ln1_b, ffn_up, ffn_conv_w, ffn_conv_b, ffn_down, ln2_g, ln2_b):
    rows = x.shape[1] // GRID_W
    lower = hg_lower_bounds(hg_lb)
    silu_c = jax.nn.silu(c)
    silu_cc = jax.nn.silu(c_ctx)
    for l in range(DEPTH):
        last = l == DEPTH - 1
        mx = jnp.split((silu_c @ ada_w[l] + ada_b[l])[:, None, :], N_MOD, axis=-1)
        mc = jnp.split(silu_cc @ ada_w[l] + ada_b[l], N_MOD, axis=-1)
        lb_f, lb_b = lower[l, 0], lower[l, 1]
        px = modulate(x, mx[0], mx[1]) @ w_in[l]
        w_in_ctx = w_in[l][:, :HG_STATE_COLS] if last else w_in[l]
        pc = modulate(ctx, mc[0], mc[1]) @ w_in_ctx
        hg_x = hg_state_inputs(px[..., :HG_STATE_COLS], lb_f, lb_b)
        hg_c = hg_state_inputs(pc[..., :HG_STATE_COLS], lb_f, lb_b)
        v_c, kf_c, kb_c, lff_c, lfb_c = hg_c
        s_f = gla_final_state(kf_c, v_c, lff_c)
        s_b = gla_final_state(flip(kb_c), flip(v_c), flip(lfb_c))
        mix_args = (hg_norm_g[l], sc_conv_w[l], sg_ln_g[l], sg_ln_b[l], sg_w[l], sg_b[l])
        mix_x = token_mixers(px, hg_x, s_f, s_b, *mix_args)
        x_new = layer_norm(ALPHA * x + mx[2] * (mix_x @ w_out[l]), ln1_g[l], ln1_b[l])
        ffn_x = conv_ffn(modulate(x_new, mx[3], mx[4]), ffn_up[l], ffn_conv_w[l], ffn_conv_b[l], ffn_down[l], rows)
        x_new = layer_norm(ALPHA * x_new + mx[5] * ffn_x, ln2_g[l], ln2_b[l])
        if not last:
            zero = jnp.zeros_like(s_f)
            mix_c = token_mixers(pc, hg_c, zero, zero, *mix_args)
            ctx = layer_norm(ALPHA * ctx + mc[2] * (mix_c @ w_out[l]), ln1_g[l], ln1_b[l])
            ffn_c = conv_ffn(modulate(ctx, mc[3], mc[4]), ffn_up[l], ffn_conv_w[l], ffn_conv_b[l], ffn_down[l], 1)
            ctx = layer_norm(ALPHA * ctx + mc[5] * ffn_c, ln2_g[l], ln2_b[l])
        x = x_new
    return x
```

```python
import functools
import math

import jax
import jax.numpy as jnp
import numpy as np
from jax import lax
from jax.experimental import pallas as pl
from jax.experimental.pallas import tpu as pltpu

F32 = jnp.float32
BF16 = jnp.bfloat16

EPS = 1e-6
TINY = 1e-30
N_MOD = 6
GRID_W = 64
HG_DK = 128
HG_CHUNK = 64
SC_TAPS = 3
SG_HEADS = 4
SG_CHUNK = 128
MOD_ROWS = 16

VMEM_LIMIT = 56 * 1024 * 1024


def _cparams(*sem):
    return pltpu.CompilerParams(dimension_semantics=sem, vmem_limit_bytes=VMEM_LIMIT)


def _ln(x):
    mu = jnp.mean(x, axis=-1, keepdims=True)
    xc = x - mu
    var = jnp.mean(xc * xc, axis=-1, keepdims=True)
    return xc * lax.rsqrt(var + EPS)


def _dot(a, b):
    return jnp.dot(a, b, preferred_element_type=F32)


def _dot_nt(a, b):
    return lax.dot_general(a, b, (((1,), (1,)), ((), ())), preferred_element_type=F32)


def _dot_tn(a, b):
    return lax.dot_general(a, b, (((0,), (0,)), ((), ())), preferred_element_type=F32)


def _resident(shape):
    nd = len(shape)
    return pl.BlockSpec(shape, lambda *_: (0,) * nd, pipeline_mode=pl.Buffered(1))


def _mods_kernel(c_ref, w_ref, b_ref, o_ref):
    c = c_ref[...]
    s = c * jax.nn.sigmoid(c)
    o_ref[...] = jnp.dot(s, w_ref[...], preferred_element_type=F32,
                         precision=lax.Precision.HIGHEST) + b_ref[...]


def _mods(cvec, ada_w, ada_b):
    depth, d, n = ada_w.shape
    tn = 1536
    return pl.pallas_call(
        _mods_kernel,
        grid=(depth, n // tn),
        in_specs=[
            pl.BlockSpec((MOD_ROWS, d), lambda l, j: (0, 0)),
            pl.BlockSpec((None, d, tn), lambda l, j: (l, 0, j)),
            pl.BlockSpec((None, 1, tn), lambda l, j: (l, 0, j)),
        ],
        out_specs=pl.BlockSpec((None, MOD_ROWS, tn), lambda l, j: (l, 0, j)),
        out_shape=jax.ShapeDtypeStruct((depth, MOD_ROWS, n), F32),
        compiler_params=_cparams("parallel", "parallel"),
        name="mods",
    )(cvec, ada_w, ada_b.reshape(depth, 1, n))


def _proj_kernel(x_ref, sh_ref, sc_ref, w_ref, o_ref):
    h = _ln(x_ref[...]) * (1.0 + sc_ref[...]) + sh_ref[...]
    o_ref[...] = _dot(h.astype(BF16), w_ref[...])


def _proj(x, shift, scale, w):
    bsz, t, d = x.shape
    n = w.shape[1]
    tm = min(512, t)
    return pl.pallas_call(
        _proj_kernel,
        grid=(bsz, t // tm),
        in_specs=[
            pl.BlockSpec((None, tm, d), lambda b, i: (b, i, 0)),
            pl.BlockSpec((None, 1, d), lambda b, i: (b, 0, 0)),
            pl.BlockSpec((None, 1, d), lambda b, i: (b, 0, 0)),
            _resident((d, n)),
        ],
        out_specs=pl.BlockSpec((None, tm, n), lambda b, i: (b, i, 0)),
        out_shape=jax.ShapeDtypeStruct((bsz, t, n), F32),
        compiler_params=_cparams("parallel", "parallel"),
        name="proj",
    )(x, shift, scale, w)


_HG_LEVELS = tuple(1 << j for j in range(int(math.log2(HG_CHUNK)) - 1, 0, -1))


def _hg_gates(z, lb):
    k = (1.0 - lb) * jax.nn.sigmoid(-z)
    logf = jnp.log(jnp.maximum(lb + (1.0 - lb) * jax.nn.sigmoid(z), TINY))
    return k, logf


def _hg_consts(fwd):
    c = HG_CHUNK
    t = lax.broadcasted_iota(jnp.int32, (c, c), 0)
    s = lax.broadcasted_iota(jnp.int32, (c, c), 1)
    valid = (t >= s) if fwd else (t <= s)
    tri = jnp.where(valid, 1.0, 0.0).astype(BF16)
    x = t ^ s
    lvl = jnp.zeros((c, c), jnp.int32)
    for j in range(1, int(math.log2(c))):
        lvl = lvl + jnp.where(x >= (1 << j), 1, 0)
    lvl = jnp.where(valid, lvl, -1)
    return tri, lvl


def _hg_ref_rows(bscr, m, fwd):
    c = HG_CHUNK
    off = m - 1 if fwd else m
    if 2 * m >= 8:
        parts = [jnp.broadcast_to(bscr[pl.ds(j * 2 * m + off, 1), :], (2 * m, HG_DK))
                 for j in range(c // (2 * m))]
    else:
        sub = lax.broadcasted_iota(jnp.int32, (8, HG_DK), 0)
        parts = []
        for j in range(c // 8):
            lo = jnp.broadcast_to(bscr[pl.ds(8 * j + off, 1), :], (8, HG_DK))
            hi = jnp.broadcast_to(bscr[pl.ds(8 * j + 4 + off, 1), :], (8, HG_DK))
            parts.append(jnp.where(sub < 4, lo, hi))
    return jnp.concatenate(parts, axis=0) if len(parts) > 1 else parts[0]


def _hg_chunk(q, k, v, g, st, bscr, consts, fwd):
    c = HG_CHUNK
    tri, lvl = consts
    g1 = g.astype(BF16)
    r1 = g - g1.astype(F32)
    g2 = r1.astype(BF16)
    g3 = (r1 - g2.astype(F32)).astype(BF16)
    b = _dot(tri, g1) + _dot(tri, g2) + _dot(tri, g3)
    bscr[...] = b
    btot = bscr[pl.ds(c - 1, 1), :] if fwd else bscr[pl.ds(0, 1), :]
    vb = v.astype(BF16)
    kx = (k * jnp.exp(btot - b)).astype(BF16)
    st_new = st * jnp.exp(btot) + _dot_tn(vb, kx)
    if q is None:
        return None, st_new
    o = _dot_nt((q * jnp.exp(b)).astype(BF16), st.astype(BF16))
    scores = jnp.zeros((c, c), F32)
    for m in _HG_LEVELS:
        x = jnp.exp(-jnp.abs(b - _hg_ref_rows(bscr, m, fwd)))
        p = _dot_nt((q * x).astype(BF16), (k * x).astype(BF16))
        scores = jnp.where(lvl == int(math.log2(m)), p, scores)
    row = lax.broadcasted_iota(jnp.int32, (c, HG_DK), 0)
    if fwd:
        d = jnp.where((row & 1) == 0, -pltpu.roll(g, c - 1, 0), 0.0)
    else:
        d = jnp.where((row & 1) == 1, -pltpu.roll(g, 1, 0), 0.0)
    p = _dot_nt((q * jnp.exp(d)).astype(BF16), (k * jnp.exp(-d)).astype(BF16))
    scores = jnp.where(lvl == 0, p, scores)
    o = o + _dot(scores.astype(BF16), vb)
    return o, st_new


def _hg_kernel(*refs, n_ctx, n_x, ctx_out):
    if ctx_out:
        (xi, xzf, xzb, xq, xg, ci, czf, czb, cq, cg, lbf_ref, lbb_ref, ng_ref,
         ox, oc, xof, xob, cof, cob, bsf, bsb) = refs
    else:
        (xi, xzf, xzb, xq, xg, ci, czf, czb, lbf_ref, lbb_ref, ng_ref,
         ox, xof, xob, bsf, bsb) = refs
        cq = cg = oc = cof = cob = None
    c = HG_CHUNK
    lbf = lbf_ref[...]
    lbb = lbb_ref[...]
    ng = ng_ref[...]
    cf = _hg_consts(True)
    cb = _hg_consts(False)
    qscale = HG_DK ** -0.5

    def silu(a):
        return a * jax.nn.sigmoid(a)

    def make_body(i_ref, zf_ref, zb_ref, q_ref, of_ref, ob_ref, n):
        def body(j, carry):
            stf, stb = carry
            rf = pl.ds(pl.multiple_of(j * c, c), c)
            rb = pl.ds(pl.multiple_of((n - 1 - j) * c, c), c)
            kf, gf = _hg_gates(zf_ref[rf, :], lbf)
            kb, gb = _hg_gates(zb_ref[rb, :], lbb)
            if q_ref is not None:
                qf = silu(q_ref[rf, :]) * qscale
                qb = silu(q_ref[rb, :]) * qscale
            else:
                qf = qb = None
            o_f, stf = _hg_chunk(qf, kf, i_ref[rf, :], gf, stf, bsf, cf, True)
            o_b, stb = _hg_chunk(qb, kb, i_ref[rb, :], gb, stb, bsb, cb, False)
            if of_ref is not None:
                of_ref[rf, :] = o_f
                ob_ref[rb, :] = o_b
            return stf, stb
        return body

    zero = jnp.zeros((HG_DK, HG_DK), F32)
    carry = lax.fori_loop(0, n_ctx, make_body(ci, czf, czb, cq, cof, cob, n_ctx), (zero, zero))
    lax.fori_loop(0, n_x, make_body(xi, xzf, xzb, xq, xof, xob, n_x), carry)

    def finish(of_ref, ob_ref, g_ref, out_ref, n_rows):
        blk = min(256, n_rows)

        def body(j, _):
            r = pl.ds(pl.multiple_of(j * blk, blk), blk)
            o = of_ref[r, :] + ob_ref[r, :]
            o = o * lax.rsqrt(jnp.mean(o * o, axis=-1, keepdims=True) + EPS) * ng
            out_ref[r, :] = (o * silu(g_ref[r, :])).astype(out_ref.dtype)
            return 0
        lax.fori_loop(0, n_rows // blk, body, 0)

    finish(xof, xob, xg, ox, n_x * c)
    if ctx_out:
        finish(cof, cob, cg, oc, n_ctx * c)


def _hgrn2(px, pc, lb_f, lb_b, norm_g, ctx_out):
    bsz, t, _ = px.shape
    tc = pc.shape[1]
    hgw = lb_f.shape[-1]
    heads = hgw // HG_DK
    c = HG_CHUNK

    def col(tt, blk):
        return pl.BlockSpec((None, tt, HG_DK), lambda b, h, blk=blk: (b, 0, blk * heads + h))

    in_specs = [col(t, j) for j in range(5)] + [col(tc, j) for j in range(5 if ctx_out else 3)]
    vec = pl.BlockSpec((1, HG_DK), lambda b, h: (0, h))
    in_specs += [vec, vec, pl.BlockSpec((1, HG_DK), lambda b, h: (0, 0))]
    args = [px] * 5 + [pc] * (5 if ctx_out else 3) + [lb_f.reshape(1, hgw), lb_b.reshape(1, hgw),
                                                        norm_g.reshape(1, HG_DK)]
    out_specs = [pl.BlockSpec((None, t, HG_DK), lambda b, h: (b, 0, h))]
    out_shape = [jax.ShapeDtypeStruct((bsz, t, hgw), BF16)]
    scratch = [pltpu.VMEM((t, HG_DK), F32), pltpu.VMEM((t, HG_DK), F32)]
    if ctx_out:
        out_specs.append(pl.BlockSpec((None, tc, HG_DK), lambda b, h: (b, 0, h)))
        out_shape.append(jax.ShapeDtypeStruct((bsz, tc, hgw), BF16))
        scratch += [pltpu.VMEM((tc, HG_DK), F32), pltpu.VMEM((tc, HG_DK), F32)]
    scratch += [pltpu.VMEM((c, HG_DK), F32), pltpu.VMEM((c, HG_DK), F32)]
    out = pl.pallas_call(
        functools.partial(_hg_kernel, n_ctx=tc // c, n_x=t // c, ctx_out=ctx_out),
        grid=(bsz, heads),
        in_specs=in_specs,
        out_specs=out_specs,
        out_shape=out_shape,
        scratch_shapes=scratch,
        compiler_params=_cparams("parallel", "parallel"),
        name="hgrn2",
    )(*args)
    return out if ctx_out else (out[0], None)


def _split_dot(a, w):
    a1 = a.astype(BF16)
    a2 = (a - a1.astype(F32)).astype(BF16)
    return _dot(a1, w) + _dot(a2, w)


def _scsg_kernel(gb_ref, gc_ref, hs_ref, u_ref, vs_ref, scw_ref, lng_ref, lnb_ref, gm_ref, wcat_ref,
                 bias_ref, bdm_ref, o_ref):
    t, scw = gb_ref.shape
    sgw = u_ref.shape[1]
    y = gc_ref[...] * hs_ref[...]
    row = lax.broadcasted_iota(jnp.int32, (t, scw), 0)
    yp = jnp.where(row == 0, 0.0, pltpu.roll(y, 1, 0))
    yn = jnp.where(row == t - 1, 0.0, pltpu.roll(y, t - 1, 0))
    w = scw_ref[...]
    sc = gb_ref[...] * (yp * w[0:1, :] + y * w[1:2, :] + yn * w[2:3, :])
    o_ref[:, 0:scw] = sc.astype(o_ref.dtype)

    gm = gm_ref[...]
    wcat = wcat_ref[...]
    bias = bias_ref[...]
    bdm = bdm_ref[...]
    lng = lng_ref[...]
    lnb = lnb_ref[...]

    def body(j, _):
        r = pl.ds(pl.multiple_of(j * SG_CHUNK, SG_CHUNK), SG_CHUNK)
        v = vs_ref[r, :]
        vc = v - _split_dot(v, gm)
        var = _split_dot(vc * vc, gm)
        vh = (vc * lax.rsqrt(var + EPS) * lng + lnb).astype(BF16)
        bd = jnp.where(bdm > 0, jnp.concatenate([vh] * SG_HEADS, axis=0), jnp.zeros_like(bdm))
        mixed = _dot(wcat, bd) + bias
        o_ref[r, scw:scw + sgw] = (u_ref[r, :] * mixed).astype(o_ref.dtype)
        return 0
    lax.fori_loop(0, t // SG_CHUNK, body, 0)


def _scsg(p, o_sc, sc_w, ln_g, ln_b, sg_w, sg_b):
    bsz, t, _ = p.shape
    scw = sc_w.shape[-1]
    sgw = ln_g.shape[-1]
    hd = sgw // SG_HEADS
    assert scw == sgw and o_sc % scw == 0
    base = o_sc // scw
    cols = [pl.BlockSpec((None, t, scw), lambda b, j=j: (b, 0, base + j)) for j in range(5)]
    gid = np.arange(sgw) // hd
    gm = jnp.asarray((gid[:, None] == gid[None, :]) / hd, BF16)
    wcat = jnp.transpose(sg_w, (1, 0, 2)).reshape(SG_CHUNK, SG_HEADS * SG_CHUNK).astype(BF16)
    bias = jnp.repeat(sg_b.T, hd, axis=1)
    bdm = jnp.asarray((np.arange(SG_HEADS * SG_CHUNK)[:, None] // SG_CHUNK) == gid[None, :], BF16)
    consts = [sc_w, ln_g.reshape(1, sgw), ln_b.reshape(1, sgw), gm, wcat, bias, bdm]
    return pl.pallas_call(
        _scsg_kernel,
        grid=(bsz,),
        in_specs=cols + [_resident(a.shape) for a in consts],
        out_specs=pl.BlockSpec((None, t, scw + sgw), lambda b: (b, 0, 0)),
        out_shape=jax.ShapeDtypeStruct((bsz, t, scw + sgw), BF16),
        compiler_params=_cparams("parallel"),
        name="scsg",
    )(p, p, p, p, p, *consts)


def _outproj_kernel(x_ref, hg_ref, sg_ref, w_ref, gate_ref, lg_ref, lb_ref, sh_ref, sc_ref,
                    xn_ref, h_ref, *, alpha):
    hgw = hg_ref.shape[1]
    y = _dot(hg_ref[...], w_ref[0:hgw, :]) + _dot(sg_ref[...], w_ref[hgw:, :])
    xn = _ln(alpha * x_ref[...] + gate_ref[...] * y) * lg_ref[...] + lb_ref[...]
    xn_ref[...] = xn
    h_ref[...] = (_ln(xn) * (1.0 + sc_ref[...]) + sh_ref[...]).astype(h_ref.dtype)


def _outproj(x, hg, scsg, w, gate, ln_g, ln_b, shift, scale, alpha):
    bsz, t, d = x.shape
    tm = min(512, t)
    row = lambda wd: pl.BlockSpec((None, tm, wd), lambda b, i: (b, i, 0))
    vec = pl.BlockSpec((None, 1, d), lambda b, i: (b, 0, 0))
    par = pl.BlockSpec((1, d), lambda b, i: (0, 0))
    return pl.pallas_call(
        functools.partial(_outproj_kernel, alpha=alpha),
        grid=(bsz, t // tm),
        in_specs=[row(d), row(hg.shape[-1]), row(scsg.shape[-1]), _resident(w.shape),
                  vec, par, par, vec, vec],
        out_specs=[row(d), row(d)],
        out_shape=[jax.ShapeDtypeStruct((bsz, t, d), F32), jax.ShapeDtypeStruct((bsz, t, d), BF16)],
        compiler_params=_cparams("parallel", "parallel"),
        name="outproj",
    )(x, hg, scsg, w, gate, ln_g.reshape(1, d), ln_b.reshape(1, d), shift, scale)


def _gelu(a):
    return 0.5 * a * (1.0 + lax.erf(a * (2.0 ** -0.5)))


def _ffn_kernel(*refs, width, halo, fchunk, alpha):
    if halo:
        (h_ref, hp_ref, hn_ref, xn_ref, wup_ref, cw_ref, cb_ref, wdn_ref, gate_ref, lg_ref, lb_ref,
         o_ref, a_scr, act_scr) = refs
    else:
        (h_ref, xn_ref, wup_ref, cw_ref, cb_ref, wdn_ref, gate_ref, lg_ref, lb_ref,
         o_ref, a_scr, act_scr) = refs
    tm = h_ref.shape[0]
    dff = wdn_ref.shape[0]
    pad = width if halo else 0
    i = pl.program_id(1)
    n_i = pl.num_programs(1)
    h = h_ref[...]
    colpos = lax.broadcasted_iota(jnp.int32, (tm, fchunk), 0) % width
    first_col = colpos == 0
    last_col = colpos == width - 1
    ext = tm + 2 * pad

    def body(c, _):
        ca = pl.ds(pl.multiple_of(c * fchunk, fchunk), fchunk)
        cg = pl.ds(pl.multiple_of(dff + c * fchunk, fchunk), fchunk)
        wa = wup_ref[:, ca]
        a_scr[pl.ds(pad, tm), :] = _dot(h, wa)
        if halo:
            keep_p = jnp.where(i > 0, 1.0, 0.0)
            keep_n = jnp.where(i < n_i - 1, 1.0, 0.0)
            a_scr[pl.ds(0, pad), :] = _dot(hp_ref[...], wa) * keep_p
            a_scr[pl.ds(pad + tm, pad), :] = _dot(hn_ref[...], wa) * keep_n
        a_ext = a_scr[...]
        a_l = pltpu.roll(a_ext, 1, 0)
        a_r = pltpu.roll(a_ext, ext - 1, 0)
        cw = cw_ref[:, ca]
        acc = jnp.zeros((tm, fchunk), F32)
        for di in range(3):
            if not halo and di != 1:
                continue
            off = di * width if halo else 0
            acc = acc + jnp.where(first_col, 0.0, a_l[off:off + tm]) * cw[3 * di:3 * di + 1, :]
            acc = acc + a_ext[off:off + tm] * cw[3 * di + 1:3 * di + 2, :]
            acc = acc + jnp.where(last_col, 0.0, a_r[off:off + tm]) * cw[3 * di + 2:3 * di + 3, :]
        gpart = _dot(h, wup_ref[:, cg])
        act_scr[:, ca] = (_gelu(acc + cb_ref[:, ca]) * gpart).astype(act_scr.dtype)
        return 0
    lax.fori_loop(0, dff // fchunk, body, 0)
    y = _dot(act_scr[...], wdn_ref[...])
    o_ref[...] = _ln(alpha * xn_ref[...] + gate_ref[...] * y) * lg_ref[...] + lb_ref[...]


def _ffn(h, xn, w_up, conv_w, conv_b, w_down, gate, ln_g, ln_b, grid_rows, alpha):
    bsz, t, d = xn.shape
    dff = w_down.shape[0]
    width = t // grid_rows
    halo = grid_rows > 1
    tm = min(512, t)
    assert tm % width == 0 and t % tm == 0
    rpt = tm // width
    fchunk = 256
    assert dff % fchunk == 0
    row = lambda wd: pl.BlockSpec((None, tm, wd), lambda b, i: (b, i, 0))
    vec = pl.BlockSpec((None, 1, d), lambda b, i: (b, 0, 0))
    par = pl.BlockSpec((1, d), lambda b, i: (0, 0))
    in_specs = [row(d)]
    args = [h]
    if halo:
        in_specs += [
            pl.BlockSpec((None, width, d), lambda b, i: (b, jnp.maximum(i * rpt - 1, 0), 0)),
            pl.BlockSpec((None, width, d), lambda b, i: (b, jnp.minimum((i + 1) * rpt, grid_rows - 1), 0)),
        ]
        args += [h, h]
    in_specs += [row(d), _resident(w_up.shape), _resident((9, dff)), _resident((1, dff)),
                 _resident(w_down.shape), vec, par, par]
    args += [xn, w_up, conv_w.reshape(9, dff), conv_b.reshape(1, dff), w_down, gate,
             ln_g.reshape(1, d), ln_b.reshape(1, d)]
    ext = tm + (2 * width if halo else 0)
    return pl.pallas_call(
        functools.partial(_ffn_kernel, width=width, halo=halo, fchunk=fchunk, alpha=alpha),
        grid=(bsz, t // tm),
        in_specs=in_specs,
        out_specs=row(d),
        out_shape=jax.ShapeDtypeStruct((bsz, t, d), F32),
        scratch_shapes=[pltpu.VMEM((ext, fchunk), F32), pltpu.VMEM((tm, dff), BF16)],
        compiler_params=_cparams("parallel", "parallel"),
        name="ffn",
    )(*args)


def _lower_bounds(lb_logits):
    p = jax.nn.softmax(lb_logits.astype(F32), axis=0)
    return jnp.cumsum(p, axis=0) - p[:1]


def kernel(x, c, ctx, c_ctx, ada_w, ada_b, w_in, hg_lb, hg_norm_g, sc_conv_w, sg_ln_g, sg_ln_b,
           sg_w, sg_b, w_out, ln1_g, ln1_b, ffn_up, ffn_conv_w, ffn_conv_b, ffn_down, ln2_g, ln2_b):
    bsz, t, d = x.shape
    depth = ada_w.shape[0]
    hgw = hg_lb.shape[-1]
    scw = sc_conv_w.shape[-1]
    o_sc = 5 * hgw
    state_cols = 3 * hgw
    alpha = (2 * depth) ** 0.25
    rows = t // GRID_W
    lower = _lower_bounds(hg_lb)

    cvec = jnp.zeros((MOD_ROWS, d), F32).at[:bsz].set(c).at[bsz].set(c_ctx)
    mods = _mods(cvec, ada_w, ada_b)

    w_in_b = w_in.astype(BF16)
    w_out_b = w_out.astype(BF16)
    up_b = ffn_up.astype(BF16)
    down_b = ffn_down.astype(BF16)

    for l in range(depth):
        last = l == depth - 1
        mx = [mods[l, :bsz, None, j * d:(j + 1) * d] for j in range(N_MOD)]
        mc = [jnp.broadcast_to(mods[l, bsz, j * d:(j + 1) * d], (bsz, 1, d)) for j in range(N_MOD)]
        px = _proj(x, mx[0], mx[1], w_in_b[l])
        pc = _proj(ctx, mc[0], mc[1], w_in_b[l][:, :state_cols] if last else w_in_b[l])
        hg_x, hg_c = _hgrn2(px, pc, lower[l, 0], lower[l, 1], hg_norm_g[l], ctx_out=not last)
        mix = (sc_conv_w[l], sg_ln_g[l], sg_ln_b[l], sg_w[l], sg_b[l])
        sg_x = _scsg(px, o_sc, *mix)
        xn, h2 = _outproj(x, hg_x, sg_x, w_out_b[l], mx[2], ln1_g[l], ln1_b[l], mx[3], mx[4], alpha)
        x_next = _ffn(h2, xn, up_b[l], ffn_conv_w[l], ffn_conv_b[l], down_b[l], mx[5],
                      ln2_g[l], ln2_b[l], rows, alpha)
        if not last:
            sg_c = _scsg(pc, o_sc, *mix)
            cn, hc2 = _outproj(ctx, hg_c, sg_c, w_out_b[l], mc[2], ln1_g[l], ln1_b[l], mc[3], mc[4], alpha)
            ctx = _ffn(hc2, cn, up_b[l], ffn_conv_w[l], ffn_conv_b[l], down_b[l], mc[5],
                       ln2_g[l], ln2_b[l], 1, alpha)
        x = x_next
    return x
```
